```python
import jax, jax.numpy as jnp
from jax import lax
import numpy as np

D_MODEL = 1024
BATCH = 16
SEQ = 4096
DEPTH = 2
DEC_BATCH = 8
DEC_SEQ = 32
PAST_LEN = 2048

CHUNK = 64
D_MIX = D_MODEL
D_POOL = D_MIX // 4
POOL_WINDOWS = (2, 4, 8, 16)
POOL_GROUP = D_POOL // len(POOL_WINDOWS)
POOL_HIST = max(POOL_WINDOWS) - 1
D_CONF = (D_MIX - D_POOL) // 2
CONF_WIDTH = 31
D_SC = D_MIX - D_POOL - D_CONF
SC_WIDTH = 3
D_IN = D_POOL + 2 * D_CONF + 3 * D_SC
PEER_HEADS = 8
PEER_NKEYS = 128
PEER_EXPERTS = PEER_NKEYS * PEER_NKEYS
PEER_TOPK = 16
PEER_DKEY = 128
PEER_DHALF = PEER_DKEY // 2
PEER_BLOCK = 128
EPS = 1e-6

kernel_name = 'hybrid_pool_conformer_shortconv_peer_step'


def rms_norm(x, g):
    xf = x.astype(jnp.float32)
    y = xf * lax.rsqrt(jnp.mean(xf * xf, axis=-1, keepdims=True) + EPS)
    return (y * g.astype(jnp.float32)).astype(x.dtype)


def layer_norm(x, g, b):
    xf = x.astype(jnp.float32)
    mu = jnp.mean(xf, axis=-1, keepdims=True)
    var = jnp.mean(jnp.square(xf - mu), axis=-1, keepdims=True)
    y = (xf - mu) * lax.rsqrt(var + EPS)
    return (y * g.astype(jnp.float32) + b.astype(jnp.float32)).astype(x.dtype)


def causal_depthwise(z, hist, w):
    width = w.shape[0]
    zp = jnp.concatenate([hist.astype(z.dtype), z], axis=1)
    y = lax.conv_general_dilated(zp, w[:, None, :].astype(z.dtype), window_strides=(1,), padding='VALID',
                                 dimension_numbers=('NWC', 'WIO', 'NWC'), feature_group_count=z.shape[-1])
    return y, zp[:, zp.shape[1] - (width - 1):]


def pool_mixer(u, hist, start, w_pool, scale):
    bsz, L, _ = u.shape
    up = jnp.concatenate([hist.astype(u.dtype), u], axis=1)
    csp = jnp.pad(jnp.cumsum(up.astype(jnp.float32), axis=1), ((0, 0), (1, 0), (0, 0)))
    pos = jnp.arange(L, dtype=jnp.float32) + start
    off = POOL_HIST + 1
    parts = []
    for g, w in enumerate(POOL_WINDOWS):
        sl = slice(g * POOL_GROUP, (g + 1) * POOL_GROUP)
        s = csp[:, off:off + L, sl] - csp[:, off - w:off - w + L, sl]
        cnt = jnp.minimum(pos + 1.0, float(w))
        parts.append(s / cnt[None, :, None])
    mean = jnp.concatenate(parts, axis=-1)
    d = (mean - u.astype(jnp.float32)).astype(u.dtype).reshape(bsz, L, len(POOL_WINDOWS), POOL_GROUP)
    y = jnp.einsum('blgc,gcd->blgd', d, w_pool).reshape(bsz, L, D_POOL) * scale
    return y, up[:, up.shape[1] - POOL_HIST:]


def mixer(h, hist_pool, hist_dw, hist_sc, start, w_in, w_pool, pool_scale, w_dw, b_dw, ln_g, ln_b, w_sc, w_out):
    p = h @ w_in
    cuts = [D_POOL, D_POOL + D_CONF, D_POOL + 2 * D_CONF, D_POOL + 2 * D_CONF + D_SC, D_POOL + 2 * D_CONF + 2 * D_SC]
    u_pool, c_val, c_gate, s_h, s_b, s_c = jnp.split(p, cuts, axis=-1)
    y_a, nh_pool = pool_mixer(u_pool, hist_pool, start, w_pool, pool_scale)
    z = c_val * jax.nn.sigmoid(c_gate)
    zc, nh_dw = causal_depthwise(z, hist_dw, w_dw)
    y_b = jax.nn.silu(layer_norm(zc + b_dw, ln_g, ln_b))
    ys, nh_sc = causal_depthwise(s_c * s_h, hist_sc, w_sc)
    y_c = s_b * ys
    out = jnp.concatenate([y_a, y_b, y_c], axis=-1) @ w_out
    return out, nh_pool, nh_dw, nh_sc


def peer(h, w_q, sub_keys, expert_u, expert_v):
    bsz, L, D = h.shape
    T = bsz * L
    t = h.reshape(T, D)
    q = (t @ w_q).reshape(T, PEER_HEADS, 2, PEER_DHALF)
    s = jnp.einsum('thpd,hpnd->thpn', q, sub_keys)
    ts, ti = lax.top_k(s, PEER_TOPK)
    cand_s = (ts[:, :, 0, :, None] + ts[:, :, 1, None, :]).reshape(T, PEER_HEADS, PEER_TOPK * PEER_TOPK)
    cand_i = (ti[:, :, 0, :, None] * PEER_NKEYS + ti[:, :, 1, None, :]).reshape(T, PEER_HEADS, PEER_TOPK * PEER_TOPK)
    best_s, best_j = lax.top_k(cand_s, PEER_TOPK)
    ids = jnp.take_along_axis(cand_i, best_j, axis=-1)
    gates = jax.nn.softmax(best_s.astype(jnp.float32), axis=-1).astype(h.dtype)
    blk = min(PEER_BLOCK, T)
    nblk = -(-T // blk)
    pad = nblk * blk - T
    tb = jnp.pad(t, ((0, pad), (0, 0))).reshape(nblk, blk, D)
    ib = jnp.pad(ids, ((0, pad), (0, 0), (0, 0))).reshape(nblk, blk, PEER_HEADS, PEER_TOPK)
    gb = jnp.pad(gates, ((0, pad), (0, 0), (0, 0))).reshape(nblk, blk, PEER_HEADS, PEER_TOPK)

    def expert_block(args):
        xb, idb, gtb = args
        act = jax.nn.gelu(jnp.einsum('td,thkd->thk', xb, expert_u[idb]), approximate=False)
        return jnp.einsum('thk,thkd->td', gtb * act, expert_v[idb])

    out = lax.map(expert_block, (tb, ib, gb)).reshape(nblk * blk, D)[:T]
    return out.reshape(bsz, L, D)


def setup_inputs(seed: int = 0) -> dict:
    key = jax.random.key(seed)
    ks = jax.random.split(key, 24)

    def nrm(k, shape, scale):
        return jax.random.normal(k, shape, jnp.float32) * scale

    return {
        'x_prompt': nrm(ks[0], (BATCH, SEQ, D_MODEL), 1.0),
        'x_sample': nrm(ks[1], (DEC_BATCH, DEC_SEQ, D_MODEL), 1.0),
        'state_pool': nrm(ks[2], (DEPTH, DEC_BATCH, POOL_HIST, D_POOL), 1.0),
        'state_dwconv': nrm(ks[3], (DEPTH, DEC_BATCH, CONF_WIDTH - 1, D_CONF), 0.5),
        'state_shortconv': nrm(ks[4], (DEPTH, DEC_BATCH, SC_WIDTH - 1, D_SC), 1.0),
        'g_mix': 1.0 + nrm(ks[5], (DEPTH, D_MODEL), 0.02),
        'w_in': nrm(ks[6], (DEPTH, D_MODEL, D_IN), D_MODEL ** -0.5),
        'w_pool': nrm(ks[7], (DEPTH, len(POOL_WINDOWS), POOL_GROUP, POOL_GROUP), POOL_GROUP ** -0.5),
        'pool_scale': 1.0 + nrm(ks[8], (DEPTH, D_POOL), 0.02),
        'w_dw': nrm(ks[9], (DEPTH, CONF_WIDTH, D_CONF), CONF_WIDTH ** -0.5),
        'b_dw': nrm(ks[10], (DEPTH, D_CONF), 0.02),
        'ln_g': 1.0 + nrm(ks[11], (DEPTH, D_CONF), 0.02),
        'ln_b': nrm(ks[12], (DEPTH, D_CONF), 0.02),
        'w_sc': nrm(ks[13], (DEPTH, SC_WIDTH, D_SC), SC_WIDTH ** -0.5),
        'w_out': nrm(ks[14], (DEPTH, D_MIX, D_MODEL), D_MIX ** -0.5),
        'g_ffn': 1.0 + nrm(ks[15], (DEPTH, D_MODEL), 0.02),
        'w_q': nrm(ks[16], (DEPTH, D_MODEL, PEER_HEADS * PEER_DKEY), D_MODEL ** -0.5),
        'sub_keys': nrm(ks[17], (DEPTH, PEER_HEADS, 2, PEER_NKEYS, PEER_DHALF), PEER_DHALF ** -0.5),
        'expert_u': nrm(ks[18], (DEPTH, PEER_EXPERTS, D_MODEL), D_MODEL ** -0.5),
        'expert_v': nrm(ks[19], (DEPTH, PEER_EXPERTS, D_MODEL), PEER_HEADS ** -0.5),
        'g_final': 1.0 + nrm(ks[20], (D_MODEL,), 0.02),
    }


def reference(x_prompt, x_sample, state_pool, state_dwconv, state_shortconv, g_mix, w_in, w_pool, pool_scale,
              w_dw, b_dw, ln_g, ln_b, w_sc, w_out, g_ffn, w_q, sub_keys, expert_u, expert_v, g_final):
    xp, xs = x_prompt, x_sample
    bp = x_prompt.shape[0]
    zp_pool = jnp.zeros((bp, POOL_HIST, D_POOL), x_prompt.dtype)
    zp_dw = jnp.zeros((bp, CONF_WIDTH - 1, D_CONF), x_prompt.dtype)
    zp_sc = jnp.zeros((bp, SC_WIDTH - 1, D_SC), x_prompt.dtype)
    pool_p, dw_p, sc_p, pool_s, dw_s, sc_s = [], [], [], [], [], []
    for l in range(DEPTH):
        m, a, b, c = mixer(rms_norm(xp, g_mix[l]), zp_pool, zp_dw, zp_sc, 0, w_in[l], w_pool[l], pool_scale[l],
                           w_dw[l], b_dw[l], ln_g[l], ln_b[l], w_sc[l], w_out[l])
        xp = xp + m
        xp = xp + peer(rms_norm(xp, g_ffn[l]), w_q[l], sub_keys[l], expert_u[l], expert_v[l])
        pool_p.append(a); dw_p.append(b); sc_p.append(c)
        m, a, b, c = mixer(rms_norm(xs, g_mix[l]), state_pool[l], state_dwconv[l], state_shortconv[l], PAST_LEN,
                           w_in[l], w_pool[l], pool_scale[l], w_dw[l], b_dw[l], ln_g[l], ln_b[l], w_sc[l], w_out[l])
        xs = xs + m
        xs = xs + peer(rms_norm(xs, g_ffn[l]), w_q[l], sub_keys[l], expert_u[l], expert_v[l])
        pool_s.append(a); dw_s.append(b); sc_s.append(c)
    y_prompt = rms_norm(xp, g_final)
    y_sample = rms_norm(xs, g_final)
    return (y_prompt, y_sample, jnp.stack(pool_p), jnp.stack(dw_p), jnp.stack(sc_p),
            jnp.stack(pool_s), jnp.stack(dw_s), jnp.stack(sc_s))
```

```python
import functools

import jax
import jax.numpy as jnp
from jax import lax
from jax.experimental import pallas as pl
from jax.experimental.pallas import tpu as pltpu

D_MODEL = 1024
CHUNK = 64
D_POOL = 256
POOL_WINDOWS = (2, 4, 8, 16)
POOL_GROUP = 64
POOL_HIST = 15
D_CONF = 384
CONF_WIDTH = 31
D_SC = 384
SC_WIDTH = 3
D_IN = D_POOL + 2 * D_CONF + 3 * D_SC
PEER_HEADS = 8
PEER_NKEYS = 128
PEER_TOPK = 16
PEER_DKEY = 128
PEER_DHALF = 64
PEER_PAIRS = PEER_HEADS * PEER_TOPK
EPS = 1e-6

SUBLANES = 8
LANES = 128
ROW_TILES = D_MODEL // LANES
VMEM_LIMIT = 48 * 1024 * 1024

F32 = jnp.float32
BF16 = jnp.bfloat16


def _rms(x, g):
    ms = jnp.mean(x * x, axis=-1, keepdims=True)
    return x * lax.rsqrt(ms + EPS) * g


POOL_PAD = 16
CONF_PAD = 32
SC_PAD = 8


def _mixer_body(start, L, x_ref, hp_ref, hd_ref, hs_ref, g_ref, win_ref, wpool_ref, pscale_ref,
                wdw_ref, bdw_ref, lng_ref, lnb_ref, wsc_ref, wout_ref,
                xo_ref, npool_ref, ndw_ref, nsc_ref, up_s, z_s, s_s):
    j = pl.program_id(1)
    nj = pl.num_programs(1)

    @pl.when(j == 0)
    def _():
        up_s[0:1, :] = jnp.zeros((1, D_POOL), F32)
        up_s[1:POOL_PAD, :] = hp_ref[0]
        z_s[0:2, :] = jnp.zeros((2, D_CONF), F32)
        z_s[2:CONF_PAD, :] = hd_ref[0]
        s_s[0:6, :] = jnp.zeros((6, D_SC), F32)
        s_s[6:SC_PAD, :] = hs_ref[0]

    x = x_ref[0]
    h = _rms(x, g_ref[...])
    p = jnp.dot(h.astype(BF16), win_ref[...], preferred_element_type=F32)
    c0, c1, c2, c3, c4 = D_POOL, D_POOL + D_CONF, D_POOL + 2 * D_CONF, D_POOL + 2 * D_CONF + D_SC, D_IN - D_SC

    u = p[:, 0:c0]
    up_s[POOL_PAD:POOL_PAD + L, :] = u
    lane = lax.broadcasted_iota(jnp.int32, (L, D_POOL), 1)
    acc = up_s[POOL_PAD:POOL_PAD + L, :] + up_s[POOL_PAD - 1:POOL_PAD - 1 + L, :]
    for jj in range(2, POOL_HIST + 1):
        first_lane = POOL_GROUP * sum(1 for w in POOL_WINDOWS if w <= jj)
        acc = acc + jnp.where(lane >= first_lane, up_s[POOL_PAD - jj:POOL_PAD - jj + L, :], 0.0)
    wlane = jnp.where(lane < POOL_GROUP, 2.0, jnp.where(lane < 2 * POOL_GROUP, 4.0,
                      jnp.where(lane < 3 * POOL_GROUP, 8.0, 16.0)))
    row = lax.broadcasted_iota(jnp.int32, (L, D_POOL), 0)
    pos = (start + j * L + row).astype(F32)
    mean = acc / jnp.minimum(pos + 1.0, wlane)
    d = mean - u
    ya = jnp.dot(d.astype(BF16), wpool_ref[...], preferred_element_type=F32) * pscale_ref[...]

    z = p[:, c0:c1] * jax.nn.sigmoid(p[:, c1:c2])
    z_s[CONF_PAD:CONF_PAD + L, :] = z
    zc = jnp.broadcast_to(bdw_ref[...], (L, D_CONF))
    for t in range(CONF_WIDTH):
        zc = zc + wdw_ref[t:t + 1, :] * z_s[2 + t:2 + t + L, :]
    mu = jnp.mean(zc, axis=-1, keepdims=True)
    zd = zc - mu
    var = jnp.mean(zd * zd, axis=-1, keepdims=True)
    yn = zd * lax.rsqrt(var + EPS) * lng_ref[...] + lnb_ref[...]
    yb = yn * jax.nn.sigmoid(yn)

    s = p[:, c4:D_IN] * p[:, c2:c3]
    s_s[SC_PAD:SC_PAD + L, :] = s
    ys = (wsc_ref[0:1, :] * s_s[6:6 + L, :] + wsc_ref[1:2, :] * s_s[7:7 + L, :]
          + wsc_ref[2:3, :] * s_s[8:8 + L, :])
    yc = p[:, c3:c4] * ys

    cat = jnp.concatenate([ya, yb, yc], axis=-1).astype(BF16)
    xo_ref[0] = x + jnp.dot(cat, wout_ref[...], preferred_element_type=F32)

    @pl.when(j == nj - 1)
    def _():
        npool_ref[0] = up_s[L + 1:L + POOL_PAD, :]
        ndw_ref[0] = z_s[L + 2:L + CONF_PAD, :]
        nsc_ref[0] = s_s[L + 6:L + SC_PAD, :]

    up_s[0:POOL_PAD, :] = up_s[L:L + POOL_PAD, :]
    z_s[0:CONF_PAD, :] = z_s[L:L + CONF_PAD, :]
    s_s[0:SC_PAD, :] = s_s[L:L + SC_PAD, :]


def _mixer(x, hp, hd, hs, start, g, win, wpool_bd, pscale, wdw, bdw, lng, lnb, wsc, wout, L):
    B, S, _ = x.shape
    assert S % L == 0 and L >= CONF_PAD and L % SUBLANES == 0
    full = lambda shape: pl.BlockSpec(shape, lambda b, j: (0,) * len(shape))
    per_b = lambda r, c: pl.BlockSpec((1, r, c), lambda b, j: (b, 0, 0))
    return pl.pallas_call(
        functools.partial(_mixer_body, start, L),
        grid=(B, S // L),
        in_specs=[
            pl.BlockSpec((1, L, D_MODEL), lambda b, j: (b, j, 0)),
            per_b(POOL_HIST, D_POOL), per_b(CONF_WIDTH - 1, D_CONF), per_b(SC_WIDTH - 1, D_SC),
            full((1, D_MODEL)), full((D_MODEL, D_IN)), full((D_POOL, D_POOL)), full((1, D_POOL)),
            full((CONF_WIDTH, D_CONF)), full((1, D_CONF)), full((1, D_CONF)), full((1, D_CONF)),
            full((SC_WIDTH, D_SC)), full((D_MODEL, D_MODEL)),
        ],
        out_specs=[
            pl.BlockSpec((1, L, D_MODEL), lambda b, j: (b, j, 0)),
            per_b(POOL_HIST, D_POOL), per_b(CONF_WIDTH - 1, D_CONF), per_b(SC_WIDTH - 1, D_SC),
        ],
        out_shape=[
            jax.ShapeDtypeStruct((B, S, D_MODEL), F32),
            jax.ShapeDtypeStruct((B, POOL_HIST, D_POOL), F32),
            jax.ShapeDtypeStruct((B, CONF_WIDTH - 1, D_CONF), F32),
            jax.ShapeDtypeStruct((B, SC_WIDTH - 1, D_SC), F32),
        ],
        scratch_shapes=[
            pltpu.VMEM((POOL_PAD + L, D_POOL), F32),
            pltpu.VMEM((CONF_PAD + L, D_CONF), F32),
            pltpu.VMEM((SC_PAD + L, D_SC), F32),
        ],
        compiler_params=pltpu.CompilerParams(
            dimension_semantics=("arbitrary", "arbitrary"), vmem_limit_bytes=VMEM_LIMIT),
        name="mixer",
    )(x, hp, hd, hs, g, win, wpool_bd, pscale, wdw, bdw, lng, lnb, wsc, wout)


def _split_bf16(a):
    hi = a.astype(BF16)
    lo = (a - hi.astype(F32)).astype(BF16)
    return hi, lo


def _dot3(a_hi, a_lo, b_hi, b_lo, dn):
    f = lambda a, b: lax.dot_general(a, b, dn, preferred_element_type=F32)
    return f(a_hi, b_hi) + f(a_hi, b_lo) + f(a_lo, b_hi)


def _topk_rows(s, k, payload=None):
    n = s.shape[0]
    rows = lax.broadcasted_iota(jnp.int32, s.shape, 0).astype(F32)
    vals, outs = [], []
    for _ in range(k):
        m = jnp.max(s, axis=0, keepdims=True)
        idx = jnp.min(jnp.where(s == m, rows, float(n)), axis=0, keepdims=True)
        hit = rows == idx
        vals.append(m)
        if payload is None:
            outs.append(idx)
        else:
            outs.append(jnp.max(jnp.where(hit, payload, -1.0), axis=0, keepdims=True))
        s = jnp.where(hit, -jnp.inf, s)
    return jnp.concatenate(vals, axis=0), jnp.concatenate(outs, axis=0)


def _route_body(x_ref, g_ref, wq_hi_ref, wq_lo_ref, k_hi_ref, k_lo_ref,
                h2_ref, ids_ref, gates_ref, q_s):
    x = x_ref[...]
    h2 = _rms(x, g_ref[...])
    h2_ref[...] = h2
    h_hi, h_lo = _split_bf16(h2)
    nt = (((1,), (1,)), ((), ()))
    q_s[...] = _dot3(wq_hi_ref[...], wq_lo_ref[...], h_hi, h_lo, nt)
    nn = (((1,), (0,)), ((), ()))

    def head(hd, carry):
        base = pl.multiple_of(hd * PEER_DKEY, PEER_DKEY)
        tops = []
        for half in range(2):
            q_hi, q_lo = _split_bf16(q_s[pl.ds(base + half * PEER_DHALF, PEER_DHALF), :])
            sc = _dot3(k_hi_ref[hd, half], k_lo_ref[hd, half], q_hi, q_lo, nn)
            tops.append(_topk_rows(sc, PEER_TOPK))
        (ts0, ti0), (ts1, ti1) = tops
        cs = jnp.concatenate([ts0[a:a + 1, :] + ts1 for a in range(PEER_TOPK)], axis=0)
        ci = jnp.concatenate([ti0[a:a + 1, :] * float(PEER_NKEYS) + ti1 for a in range(PEER_TOPK)], axis=0)
        bs, bi = _topk_rows(cs, PEER_TOPK, payload=ci)
        e = jnp.exp(bs - bs[0:1, :])
        gates = e / jnp.sum(e, axis=0, keepdims=True)
        r0 = pl.multiple_of(hd * PEER_TOPK, PEER_TOPK)
        ids_ref[pl.ds(r0, PEER_TOPK), :] = bi.astype(jnp.int32)
        gates_ref[pl.ds(r0, PEER_TOPK), :] = gates
        return carry

    lax.fori_loop(0, PEER_HEADS, head, 0)


def _route(x, g, wq_hi, wq_lo, k_hi, k_lo, TT):
    T = x.shape[0]
    assert T % TT == 0 and TT % LANES == 0
    full = lambda shape: pl.BlockSpec(shape, lambda i: (0,) * len(shape))
    return pl.pallas_call(
        _route_body,
        grid=(T // TT,),
        in_specs=[
            pl.BlockSpec((TT, D_MODEL), lambda i: (i, 0)),
            full((1, D_MODEL)), full((D_MODEL, D_MODEL)), full((D_MODEL, D_MODEL)),
            full((PEER_HEADS, 2, PEER_NKEYS, PEER_DHALF)), full((PEER_HEADS, 2, PEER_NKEYS, PEER_DHALF)),
        ],
        out_specs=[
            pl.BlockSpec((TT, D_MODEL), lambda i: (i, 0)),
            pl.BlockSpec((PEER_PAIRS, TT), lambda i: (0, i)),
            pl.BlockSpec((PEER_PAIRS, TT), lambda i: (0, i)),
        ],
        out_shape=[
            jax.ShapeDtypeStruct((T, D_MODEL), F32),
            jax.ShapeDtypeStruct((PEER_PAIRS, T), jnp.int32),
            jax.ShapeDtypeStruct((PEER_PAIRS, T), F32),
        ],
        scratch_shapes=[pltpu.VMEM((PEER_HEADS * PEER_DKEY, TT), F32)],
        compiler_params=pltpu.CompilerParams(
            dimension_semantics=("arbitrary",), vmem_limit_bytes=VMEM_LIMIT),
        name="peer_route",
    )(x, g, wq_hi, wq_lo, k_hi, k_lo)


def _expert_row(tab_ref, e):
    r = pl.multiple_of(e * ROW_TILES, ROW_TILES)
    return tab_ref[pl.ds(r, ROW_TILES), :].astype(F32)


def _peer_act_body(TT, ids_ref, x3_ref, gT_ref, tab_ref, wT_ref, p_s):
    lane = lax.broadcasted_iota(jnp.int32, (PEER_PAIRS, TT), 1)

    def token(t, sT):
        xt = x3_ref[t]
        base = t * PEER_PAIRS
        for k in range(PEER_PAIRS):
            p_s[k * ROW_TILES:(k + 1) * ROW_TILES, :] = _expert_row(tab_ref, ids_ref[0, 0, base + k]) * xt
        q = p_s[pl.ds(0, PEER_PAIRS, stride=ROW_TILES), :]
        for c in range(1, ROW_TILES):
            q = q + p_s[pl.ds(c, PEER_PAIRS, stride=ROW_TILES), :]
        col = jnp.sum(q, axis=1, keepdims=True)
        return jnp.where(lane == t, col, sT)

    sT = lax.fori_loop(0, TT, token, jnp.zeros((PEER_PAIRS, TT), F32))
    gelu = 0.5 * sT * (1.0 + lax.erf(sT * (2.0 ** -0.5)))
    wT_ref[...] = gT_ref[...] * gelu


def _peer_act(ids_tm, x3, gT, tab, TT):
    T = x3.shape[0]
    assert T % TT == 0 and TT % LANES == 0
    ids_blk = ids_tm.reshape(T // TT, 1, TT * PEER_PAIRS)
    return pl.pallas_call(
        functools.partial(_peer_act_body, TT),
        grid=(T // TT,),
        in_specs=[
            pl.BlockSpec((1, 1, TT * PEER_PAIRS), lambda i: (i, 0, 0), memory_space=pltpu.SMEM),
            pl.BlockSpec((TT, ROW_TILES, LANES), lambda i: (i, 0, 0)),
            pl.BlockSpec((PEER_PAIRS, TT), lambda i: (0, i)),
            pl.BlockSpec(memory_space=pltpu.VMEM),
        ],
        out_specs=pl.BlockSpec((PEER_PAIRS, TT), lambda i: (0, i)),
        out_shape=jax.ShapeDtypeStruct((PEER_PAIRS, T), F32),
        scratch_shapes=[pltpu.VMEM((PEER_PAIRS * ROW_TILES, LANES), F32)],
        compiler_params=pltpu.CompilerParams(
            dimension_semantics=("arbitrary",), vmem_limit_bytes=VMEM_LIMIT),
        name="peer_act",
    )(ids_blk, x3, gT, tab)


N_ACC = 4


def _peer_out_body(TT, ids_ref, w_ref, x3_ref, tab_ref, o_ref):
    def token(t, carry):
        base = t * PEER_PAIRS
        accs = [x3_ref[t]] + [jnp.zeros((ROW_TILES, LANES), F32)] * (N_ACC - 1)
        for k in range(PEER_PAIRS):
            accs[k % N_ACC] = accs[k % N_ACC] + w_ref[0, 0, base + k] * _expert_row(tab_ref, ids_ref[0, 0, base + k])
        o_ref[t] = (accs[0] + accs[1]) + (accs[2] + accs[3])
        return carry

    lax.fori_loop(0, TT, token, 0)


def _peer_out(ids_tm, w_tm, x3, tab, TT):
    T = x3.shape[0]
    assert T % TT == 0
    ids_blk = ids_tm.reshape(T // TT, 1, TT * PEER_PAIRS)
    w_blk = w_tm.reshape(T // TT, 1, TT * PEER_PAIRS)
    smem = pl.BlockSpec((1, 1, TT * PEER_PAIRS), lambda i: (i, 0, 0), memory_space=pltpu.SMEM)
    return pl.pallas_call(
        functools.partial(_peer_out_body, TT),
        grid=(T // TT,),
        in_specs=[
            smem, smem,
            pl.BlockSpec((TT, ROW_TILES, LANES), lambda i: (i, 0, 0)),
            pl.BlockSpec(memory_space=pltpu.VMEM),
        ],
        out_specs=pl.BlockSpec((TT, ROW_TILES, LANES), lambda i: (i, 0, 0)),
        out_shape=jax.ShapeDtypeStruct((T, ROW_TILES, LANES), F32),
        compiler_params=pltpu.CompilerParams(
            dimension_semantics=("arbitrary",), vmem_limit_bytes=VMEM_LIMIT),
        name="peer_out",
    )(ids_blk, w_blk, x3, tab)


def _final_norm_body(x_ref, g_ref, o_ref):
    o_ref[...] = _rms(x_ref[...], g_ref[...])


def _final_norm(x, g, TT):
    T = x.shape[0]
    return pl.pallas_call(
        _final_norm_body,
        grid=(T // TT,),
        in_specs=[pl.BlockSpec((TT, D_MODEL), lambda i: (i, 0)), pl.BlockSpec((1, D_MODEL), lambda i: (0, 0))],
        out_specs=pl.BlockSpec((TT, D_MODEL), lambda i: (i, 0)),
        out_shape=jax.ShapeDtypeStruct((T, D_MODEL), F32),
        name="final_norm",
    )(x, g)


def _block_diag(w_pool):
    n = w_pool.shape[0]
    out = jnp.zeros((n * POOL_GROUP, n * POOL_GROUP), w_pool.dtype)
    for gi in range(n):
        out = out.at[gi * POOL_GROUP:(gi + 1) * POOL_GROUP, gi * POOL_GROUP:(gi + 1) * POOL_GROUP].set(w_pool[gi])
    return out


def _peer(x, g, wq_hi, wq_lo, k_hi, k_lo, tab_u, tab_v, tt_route, tt_act, tt_out):
    T = x.shape[0]
    h2, idsT, gatesT = _route(x, g, wq_hi, wq_lo, k_hi, k_lo, tt_route)
    ids_tm = idsT.T
    wT = _peer_act(ids_tm, h2.reshape(T, ROW_TILES, LANES), gatesT, tab_u, tt_act)
    out3 = _peer_out(ids_tm, wT.T, x.reshape(T, ROW_TILES, LANES), tab_v, tt_out)
    return out3.reshape(T, D_MODEL)


def kernel(x_prompt, x_sample, state_pool, state_dwconv, state_shortconv, g_mix, w_in, w_pool, pool_scale, w_dw, b_dw, ln_g, ln_b, w_sc, w_out, g_ffn, w_q, sub_keys, expert_u, expert_v, g_final):
    depth = w_in.shape[0]
    bp, sp, _ = x_prompt.shape
    bs, ss, _ = x_sample.shape
    past_len = 2048
    n_exp = expert_u.shape[1]
    xp, xs = x_prompt, x_sample
    zp_pool = jnp.zeros((bp, POOL_HIST, D_POOL), F32)
    zp_dw = jnp.zeros((bp, CONF_WIDTH - 1, D_CONF), F32)
    zp_sc = jnp.zeros((bp, SC_WIDTH - 1, D_SC), F32)
    states = [[] for _ in range(6)]
    for l in range(depth):
        mix_w = (g_mix[l][None, :], w_in[l].astype(BF16), _block_diag(w_pool[l]).astype(BF16), pool_scale[l][None, :],
                 w_dw[l], b_dw[l][None, :], ln_g[l][None, :], ln_b[l][None, :], w_sc[l], w_out[l].astype(BF16))
        wq_hi, wq_lo = _split_bf16(w_q[l].T)
        k_hi, k_lo = _split_bf16(sub_keys[l])
        tab_u = expert_u[l].astype(BF16).reshape(n_exp * ROW_TILES, LANES)
        tab_v = expert_v[l].astype(BF16).reshape(n_exp * ROW_TILES, LANES)
        peer_w = (g_ffn[l][None, :], wq_hi, wq_lo, k_hi, k_lo, tab_u, tab_v)

        xp, a, b, c = _mixer(xp, zp_pool, zp_dw, zp_sc, 0, *mix_w, L=256)
        xp = _peer(xp.reshape(bp * sp, D_MODEL), *peer_w, tt_route=256, tt_act=128, tt_out=64).reshape(bp, sp, D_MODEL)
        states[0].append(a); states[1].append(b); states[2].append(c)

        xs, a, b, c = _mixer(xs, state_pool[l], state_dwconv[l], state_shortconv[l], past_len, *mix_w, L=ss)
        xs = _peer(xs.reshape(bs * ss, D_MODEL), *peer_w, tt_route=256, tt_act=128, tt_out=64).reshape(bs, ss, D_MODEL)
        states[3].append(a); states[4].append(b); states[5].append(c)

    y_prompt = _final_norm(xp.reshape(bp * sp, D_MODEL), g_final[None, :], 512).reshape(bp, sp, D_MODEL)
    y_sample = _final_norm(xs.reshape(bs * ss, D_MODEL), g_final[None, :], bs * ss).reshape(bs, ss, D_MODEL)
    return (y_prompt, y_sample) + tuple(jnp.stack(s) for s in states)
```

```python
import functools

import jax
import jax.numpy as jnp
from jax import lax
from jax.experimental import pallas as pl
from jax.experimental.pallas import tpu as pltpu

D_MODEL = 1024
PAST_LEN = 2048
D_POOL = 256
POOL_WINDOWS = (2, 4, 8, 16)
POOL_GROUP = 64
POOL_HIST = 15
D_CONF = 384
CONF_WIDTH = 31
D_SC = 384
SC_WIDTH = 3
D_IN = D_POOL + 2 * D_CONF + 3 * D_SC
PEER_HEADS = 8
PEER_NKEYS = 128
PEER_TOPK = 16
PEER_DKEY = 128
PEER_DHALF = 64
PEER_PAIRS = PEER_HEADS * PEER_TOPK
EPS = 1e-6

SUBLANES = 8
LANES = 128
ROW_TILES = D_MODEL // LANES
TAB_ROWS = ROW_TILES // 2
VMEM_LIMIT = 48 * 1024 * 1024

F32 = jnp.float32
BF16 = jnp.bfloat16


def _tiles(seq_len, n_tokens):
    return dict(mixer=min(256, seq_len), route=256, expert=LANES, norm=min(512, n_tokens))


def _rms(x, g):
    ms = jnp.mean(x * x, axis=-1, keepdims=True)
    return x * lax.rsqrt(ms + EPS) * g


POOL_PAD = 16
CONF_PAD = 32
SC_PAD = 8


def _mixer_body(start, L, x_ref, hp_ref, hd_ref, hs_ref, g_ref, win_ref, wpool_ref, pscale_ref,
                wdw_ref, bdw_ref, lng_ref, lnb_ref, wsc_ref, wout_ref,
                xo_ref, npool_ref, ndw_ref, nsc_ref, up_s, z_s, s_s):
    j = pl.program_id(1)
    nj = pl.num_programs(1)

    @pl.when(j == 0)
    def _():
        up_s[0:1, :] = jnp.zeros((1, D_POOL), F32)
        up_s[1:POOL_PAD, :] = hp_ref[0]
        z_s[0:2, :] = jnp.zeros((2, D_CONF), F32)
        z_s[2:CONF_PAD, :] = hd_ref[0]
        s_s[0:6, :] = jnp.zeros((6, D_SC), F32)
        s_s[6:SC_PAD, :] = hs_ref[0]

    x = x_ref[0]
    h = _rms(x, g_ref[...])
    p = jnp.dot(h.astype(BF16), win_ref[...], preferred_element_type=F32)
    c0, c1, c2, c3, c4 = D_POOL, D_POOL + D_CONF, D_POOL + 2 * D_CONF, D_POOL + 2 * D_CONF + D_SC, D_IN - D_SC

    u = p[:, 0:c0]
    up_s[POOL_PAD:POOL_PAD + L, :] = u
    lane = lax.broadcasted_iota(jnp.int32, (L, D_POOL), 1)
    acc = up_s[POOL_PAD:POOL_PAD + L, :] + up_s[POOL_PAD - 1:POOL_PAD - 1 + L, :]
    for jj in range(2, POOL_HIST + 1):
        first_lane = POOL_GROUP * sum(1 for w in POOL_WINDOWS if w <= jj)
        acc = acc + jnp.where(lane >= first_lane, up_s[POOL_PAD - jj:POOL_PAD - jj + L, :], 0.0)
    wlane = jnp.where(lane < POOL_GROUP, 2.0, jnp.where(lane < 2 * POOL_GROUP, 4.0,
                      jnp.where(lane < 3 * POOL_GROUP, 8.0, 16.0)))
    row = lax.broadcasted_iota(jnp.int32, (L, D_POOL), 0)
    pos = (start + j * L + row).astype(F32)
    mean = acc / jnp.minimum(pos + 1.0, wlane)
    d = mean - u
    ya = jnp.dot(d.astype(BF16), wpool_ref[...], preferred_element_type=F32) * pscale_ref[...]

    z = p[:, c0:c1] * jax.nn.sigmoid(p[:, c1:c2])
    z_s[CONF_PAD:CONF_PAD + L, :] = z
    zc = jnp.broadcast_to(bdw_ref[...], (L, D_CONF))
    for t in range(CONF_WIDTH):
        zc = zc + wdw_ref[t:t + 1, :] * z_s[2 + t:2 + t + L, :]
    mu = jnp.mean(zc, axis=-1, keepdims=True)
    zd = zc - mu
    var = jnp.mean(zd * zd, axis=-1, keepdims=True)
    yn = zd * lax.rsqrt(var + EPS) * lng_ref[...] + lnb_ref[...]
    yb = yn * jax.nn.sigmoid(yn)

    s = p[:, c4:D_IN] * p[:, c2:c3]
    s_s[SC_PAD:SC_PAD + L, :] = s
    ys = (wsc_ref[0:1, :] * s_s[6:6 + L, :] + wsc_ref[1:2, :] * s_s[7:7 + L, :]
          + wsc_ref[2:3, :] * s_s[8:8 + L, :])
    yc = p[:, c3:c4] * ys

    cat = jnp.concatenate([ya, yb, yc], axis=-1).astype(BF16)
    xo_ref[0] = x + jnp.dot(cat, wout_ref[...], preferred_element_type=F32)

    @pl.when(j == nj - 1)
    def _():
        npool_ref[0] = up_s[L + 1:L + POOL_PAD, :]
        ndw_ref[0] = z_s[L + 2:L + CONF_PAD, :]
        nsc_ref[0] = s_s[L + 6:L + SC_PAD, :]

    up_s[0:POOL_PAD, :] = up_s[L:L + POOL_PAD, :]
    z_s[0:CONF_PAD, :] = z_s[L:L + CONF_PAD, :]
    s_s[0:SC_PAD, :] = s_s[L:L + SC_PAD, :]


def _mixer(x, hp, hd, hs, start, g, win, wpool_bd, pscale, wdw, bdw, lng, lnb, wsc, wout, L):
    B, S, _ = x.shape
    assert S % L == 0 and L >= CONF_PAD and L % SUBLANES == 0
    full = lambda shape: pl.BlockSpec(shape, lambda b, j: (0,) * len(shape))
    per_b = lambda r, c: pl.BlockSpec((1, r, c), lambda b, j: (b, 0, 0))
    return pl.pallas_call(
        functools.partial(_mixer_body, start, L),
        grid=(B, S // L),
        in_specs=[
            pl.BlockSpec((1, L, D_MODEL), lambda b, j: (b, j, 0)),
            per_b(POOL_HIST, D_POOL), per_b(CONF_WIDTH - 1, D_CONF), per_b(SC_WIDTH - 1, D_SC),
            full((1, D_MODEL)), full((D_MODEL, D_IN)), full((D_POOL, D_POOL)), full((1, D_POOL)),
            full((CONF_WIDTH, D_CONF)), full((1, D_CONF)), full((1, D_CONF)), full((1, D_CONF)),
            full((SC_WIDTH, D_SC)), full((D_MODEL, D_MODEL)),
        ],
        out_specs=[
            pl.BlockSpec((1, L, D_MODEL), lambda b, j: (b, j, 0)),
            per_b(POOL_HIST, D_POOL), per_b(CONF_WIDTH - 1, D_CONF), per_b(SC_WIDTH - 1, D_SC),
        ],
        out_shape=[
            jax.ShapeDtypeStruct((B, S, D_MODEL), F32),
            jax.ShapeDtypeStruct((B, POOL_HIST, D_POOL), F32),
            jax.ShapeDtypeStruct((B, CONF_WIDTH - 1, D_CONF), F32),
            jax.ShapeDtypeStruct((B, SC_WIDTH - 1, D_SC), F32),
        ],
        scratch_shapes=[
            pltpu.VMEM((POOL_PAD + L, D_POOL), F32),
            pltpu.VMEM((CONF_PAD + L, D_CONF), F32),
            pltpu.VMEM((SC_PAD + L, D_SC), F32),
        ],
        compiler_params=pltpu.CompilerParams(
            dimension_semantics=("arbitrary", "arbitrary"), vmem_limit_bytes=VMEM_LIMIT),
        name="mixer",
    )(x, hp, hd, hs, g, win, wpool_bd, pscale, wdw, bdw, lng, lnb, wsc, wout)


def _split_bf16(a):
    hi = a.astype(BF16)
    lo = (a - hi.astype(F32)).astype(BF16)
    return hi, lo


def _dot3(a_hi, a_lo, b_hi, b_lo, dn):
    f = lambda a, b: lax.dot_general(a, b, dn, preferred_element_type=F32)
    return f(a_hi, b_hi) + f(a_hi, b_lo) + f(a_lo, b_hi)


def _topk_rows(s, k, payload=None):
    n = s.shape[0]
    rows = lax.broadcasted_iota(jnp.int32, s.shape, 0).astype(F32)
    vals, outs = [], []
    for _ in range(k):
        m = jnp.max(s, axis=0, keepdims=True)
        idx = jnp.min(jnp.where(s == m, rows, float(n)), axis=0, keepdims=True)
        hit = rows == idx
        vals.append(m)
        if payload is None:
            outs.append(idx)
        else:
            outs.append(jnp.max(jnp.where(hit, payload, -1.0), axis=0, keepdims=True))
        s = jnp.where(hit, -jnp.inf, s)
    return jnp.concatenate(vals, axis=0), jnp.concatenate(outs, axis=0)


def _staircase():
    return [(a, b) for a in range(PEER_TOPK) for b in range(PEER_TOPK) if (a + 1) * (b + 1) <= PEER_TOPK]


def _route_body(x_ref, g_ref, wq_hi_ref, wq_lo_ref, k_hi_ref, k_lo_ref,
                h2_ref, ids_ref, gates_ref, q_s):
    x = x_ref[...]
    h2 = _rms(x, g_ref[...])
    h2_ref[...] = h2
    h_hi, h_lo = _split_bf16(h2)
    nt = (((1,), (1,)), ((), ()))
    q_s[...] = _dot3(wq_hi_ref[...], wq_lo_ref[...], h_hi, h_lo, nt)
    nn = (((1,), (0,)), ((), ()))
    cand = _staircase()
    n_pad = -len(cand) % SUBLANES
    tt = x.shape[0]

    def head(hd, carry):
        base = pl.multiple_of(hd * PEER_DKEY, PEER_DKEY)
        r0 = pl.multiple_of(hd * PEER_TOPK, PEER_TOPK)
        for cb in range(tt // LANES):
            cols = slice(cb * LANES, (cb + 1) * LANES)
            tops = []
            for half in range(2):
                q_hi, q_lo = _split_bf16(q_s[pl.ds(base + half * PEER_DHALF, PEER_DHALF), cols])
                sc = _dot3(k_hi_ref[hd, half], k_lo_ref[hd, half], q_hi, q_lo, nn)
                tops.append(_topk_rows(sc, PEER_TOPK))
            (ts0, ti0), (ts1, ti1) = tops
            cs = jnp.concatenate([ts0[a:a + 1, :] + ts1[b:b + 1, :] for a, b in cand]
                                 + [jnp.full((n_pad, LANES), -jnp.inf, F32)], axis=0)
            ci = jnp.concatenate([ti0[a:a + 1, :] * float(PEER_NKEYS) + ti1[b:b + 1, :] for a, b in cand]
                                 + [jnp.full((n_pad, LANES), -1.0, F32)], axis=0)
            bs, bi = _topk_rows(cs, PEER_TOPK, payload=ci)
            e = jnp.exp(bs - bs[0:1, :])
            gates = e / jnp.sum(e, axis=0, keepdims=True)
            ids_ref[pl.ds(r0, PEER_TOPK), cols] = bi.astype(jnp.int32) * TAB_ROWS
            gates_ref[pl.ds(r0, PEER_TOPK), cols] = gates
        return carry

    lax.fori_loop(0, PEER_HEADS, head, 0)


def _route(x, g, wq_hi, wq_lo, k_hi, k_lo, TT):
    T = x.shape[0]
    assert T % TT == 0 and TT % LANES == 0
    full = lambda shape: pl.BlockSpec(shape, lambda i: (0,) * len(shape))
    return pl.pallas_call(
        _route_body,
        grid=(T // TT,),
        in_specs=[
            pl.BlockSpec((TT, D_MODEL), lambda i: (i, 0)),
            full((1, D_MODEL)), full((D_MODEL, D_MODEL)), full((D_MODEL, D_MODEL)),
            full((PEER_HEADS, 2, PEER_NKEYS, PEER_DHALF)), full((PEER_HEADS, 2, PEER_NKEYS, PEER_DHALF)),
        ],
        out_specs=[
            pl.BlockSpec((TT, D_MODEL), lambda i: (i, 0)),
            pl.BlockSpec((PEER_PAIRS, TT), lambda i: (0, i)),
            pl.BlockSpec((PEER_PAIRS, TT), lambda i: (0, i)),
        ],
        out_shape=[
            jax.ShapeDtypeStruct((T, D_MODEL), F32),
            jax.ShapeDtypeStruct((PEER_PAIRS, T), jnp.int32),
            jax.ShapeDtypeStruct((PEER_PAIRS, T), F32),
        ],
        scratch_shapes=[pltpu.VMEM((PEER_HEADS * PEER_DKEY, TT), F32)],
        compiler_params=pltpu.CompilerParams(
            dimension_semantics=("arbitrary",), vmem_limit_bytes=VMEM_LIMIT),
        name="peer_route",
    )(x, g, wq_hi, wq_lo, k_hi, k_lo)


def _expert_row(tab_ref, e4):
    r = pl.multiple_of(e4, TAB_ROWS)
    return pltpu.bitcast(tab_ref[pl.ds(r, TAB_ROWS), :], BF16).astype(F32)


def _gather_products(tab_ref, ids_ref, t, xt, p_ref):
    for k in range(PEER_PAIRS):
        p = _expert_row(tab_ref, ids_ref[0, t, k]) * xt
        p = p + pltpu.roll(p, TAB_ROWS, 0)
        p_ref[k * TAB_ROWS:(k + 1) * TAB_ROWS, :] = p[0:TAB_ROWS, :]


def _pair_sums(p_ref):
    q = p_ref[pl.ds(0, PEER_PAIRS, stride=TAB_ROWS), :]
    for c in range(1, TAB_ROWS):
        q = q + p_ref[pl.ds(c, PEER_PAIRS, stride=TAB_ROWS), :]
    return jnp.sum(q, axis=1, keepdims=True)


def _peer_act_body(TT, ids_ref, x3_ref, gT_ref, tab_ref, wT_ref, pa_s, pb_s):
    lane = lax.broadcasted_iota(jnp.int32, (PEER_PAIRS, TT), 1)
    pb_s[...] = jnp.zeros(pb_s.shape, F32)

    def two_tokens(i, sT):
        t0 = 2 * i
        _gather_products(tab_ref, ids_ref, t0, x3_ref[t0], pa_s)
        sT = jnp.where(lane == t0 - 1, _pair_sums(pb_s), sT)
        _gather_products(tab_ref, ids_ref, t0 + 1, x3_ref[t0 + 1], pb_s)
        return jnp.where(lane == t0, _pair_sums(pa_s), sT)

    sT = lax.fori_loop(0, TT // 2, two_tokens, jnp.zeros((PEER_PAIRS, TT), F32))
    sT = jnp.where(lane == TT - 1, _pair_sums(pb_s), sT)
    gelu = 0.5 * sT * (1.0 + lax.erf(sT * (2.0 ** -0.5)))
    wT_ref[...] = gT_ref[...] * gelu


def _peer_act(ids_tm, x3, gT, tab, TT):
    T = x3.shape[0]
    assert T % TT == 0 and TT % LANES == 0
    ids_blk = ids_tm.reshape(T // TT, TT, PEER_PAIRS)
    return pl.pallas_call(
        functools.partial(_peer_act_body, TT),
        grid=(T // TT,),
        in_specs=[
            pl.BlockSpec((1, TT, PEER_PAIRS), lambda i: (i, 0, 0), memory_space=pltpu.SMEM),
            pl.BlockSpec((TT, ROW_TILES, LANES), lambda i: (i, 0, 0)),
            pl.BlockSpec((PEER_PAIRS, TT), lambda i: (0, i)),
            pl.BlockSpec(memory_space=pltpu.VMEM),
        ],
        out_specs=pl.BlockSpec((PEER_PAIRS, TT), lambda i: (0, i)),
        out_shape=jax.ShapeDtypeStruct((PEER_PAIRS, T), F32),
        scratch_shapes=[pltpu.VMEM((PEER_PAIRS * TAB_ROWS, LANES), F32)] * 2,
        compiler_params=pltpu.CompilerParams(
            dimension_semantics=("arbitrary",), vmem_limit_bytes=VMEM_LIMIT),
        name="peer_act",
    )(ids_blk, x3, gT, tab)


N_ACC = 4


def _expand_weights(wT_ref, lane, t, w_ref):
    wcol = jnp.sum(jnp.where(lane == t, wT_ref[...], 0.0), axis=1, keepdims=True)
    w_ref[...] = jnp.broadcast_to(wcol, (PEER_PAIRS, LANES))


def _weighted_rows(tab_ref, ids_ref, t, w_ref, x_t):
    accs = [x_t] + [jnp.zeros((ROW_TILES, LANES), F32)] * (N_ACC - 1)
    for k in range(PEER_PAIRS):
        w = jnp.broadcast_to(w_ref[k:k + 1, :], (ROW_TILES, LANES))
        accs[k % N_ACC] = accs[k % N_ACC] + w * _expert_row(tab_ref, ids_ref[0, t, k])
    return (accs[0] + accs[1]) + (accs[2] + accs[3])


def _peer_out_body(TT, ids_ref, wT_ref, x3_ref, tab_ref, o_ref, wa_s, wb_s):
    lane = lax.broadcasted_iota(jnp.int32, (PEER_PAIRS, TT), 1)
    _expand_weights(wT_ref, lane, 0, wa_s)

    def two_tokens(i, carry):
        t0 = 2 * i
        _expand_weights(wT_ref, lane, t0 + 1, wb_s)
        o_ref[t0] = _weighted_rows(tab_ref, ids_ref, t0, wa_s, x3_ref[t0])
        _expand_weights(wT_ref, lane, t0 + 2, wa_s)
        o_ref[t0 + 1] = _weighted_rows(tab_ref, ids_ref, t0 + 1, wb_s, x3_ref[t0 + 1])
        return carry

    lax.fori_loop(0, TT // 2, two_tokens, 0)


def _peer_out(ids_tm, wT, x3, tab, TT):
    T = x3.shape[0]
    assert T % TT == 0 and TT % LANES == 0
    ids_blk = ids_tm.reshape(T // TT, TT, PEER_PAIRS)
    return pl.pallas_call(
        functools.partial(_peer_out_body, TT),
        grid=(T // TT,),
        in_specs=[
            pl.BlockSpec((1, TT, PEER_PAIRS), lambda i: (i, 0, 0), memory_space=pltpu.SMEM),
            pl.BlockSpec((PEER_PAIRS, TT), lambda i: (0, i)),
            pl.BlockSpec((TT, ROW_TILES, LANES), lambda i: (i, 0, 0)),
            pl.BlockSpec(memory_space=pltpu.VMEM),
        ],
        out_specs=pl.BlockSpec((TT, ROW_TILES, LANES), lambda i: (i, 0, 0)),
        out_shape=jax.ShapeDtypeStruct((T, ROW_TILES, LANES), F32),
        scratch_shapes=[pltpu.VMEM((PEER_PAIRS, LANES), F32)] * 2,
        compiler_params=pltpu.CompilerParams(
            dimension_semantics=("arbitrary",), vmem_limit_bytes=VMEM_LIMIT),
        name="peer_out",
    )(ids_blk, wT, x3, tab)


def _final_norm_body(x_ref, g_ref, o_ref):
    o_ref[...] = _rms(x_ref[...], g_ref[...])


def _final_norm(x, g, TT):
    T = x.shape[0]
    assert T % TT == 0
    return pl.pallas_call(
        _final_norm_body,
        grid=(T // TT,),
        in_specs=[pl.BlockSpec((TT, D_MODEL), lambda i: (i, 0)), pl.BlockSpec((1, D_MODEL), lambda i: (0, 0))],
        out_specs=pl.BlockSpec((TT, D_MODEL), lambda i: (i, 0)),
        out_shape=jax.ShapeDtypeStruct((T, D_MODEL), F32),
        name="final_norm",
    )(x, g)


def _block_diag(w_pool):
    n = w_pool.shape[0]
    out = jnp.zeros((n * POOL_GROUP, n * POOL_GROUP), w_pool.dtype)
    for gi in range(n):
        out = out.at[gi * POOL_GROUP:(gi + 1) * POOL_GROUP, gi * POOL_GROUP:(gi + 1) * POOL_GROUP].set(w_pool[gi])
    return out


def _pack_table(w):
    n = w.shape[0]
    bits = lax.bitcast_convert_type(w.astype(BF16), jnp.uint16).astype(jnp.uint32).reshape(n, TAB_ROWS, 2, LANES)
    words = bits[:, :, 0, :] | (bits[:, :, 1, :] << 16)
    return lax.bitcast_convert_type(words, jnp.int32).reshape(n * TAB_ROWS, LANES)


def _peer(x, g, wq_hi, wq_lo, k_hi, k_lo, tab_u, tab_v, tiles):
    T = x.shape[0]
    h2, idsT, gatesT = _route(x, g, wq_hi, wq_lo, k_hi, k_lo, tiles["route"])
    ids_tm = idsT.T
    wT = _peer_act(ids_tm, h2.reshape(T, ROW_TILES, LANES), gatesT, tab_u, tiles["expert"])
    out3 = _peer_out(ids_tm, wT, x.reshape(T, ROW_TILES, LANES), tab_v, tiles["expert"])
    return out3.reshape(T, D_MODEL)


def kernel(x_prompt, x_sample, state_pool, state_dwconv, state_shortconv, g_mix, w_in, w_pool, pool_scale, w_dw, b_dw, ln_g, ln_b, w_sc, w_out, g_ffn, w_q, sub_keys, expert_u, expert_v, g_final):
    depth = w_in.shape[0]
    bp, sp, _ = x_prompt.shape
    bs, ss, _ = x_sample.shape
    tiles_p, tiles_s = _tiles(sp, bp * sp), _tiles(ss, bs * ss)
    xp, xs = x_prompt, x_sample
    zp_pool = jnp.zeros((bp, POOL_HIST, D_POOL), F32)
    zp_dw = jnp.zeros((bp, CONF_WIDTH - 1, D_CONF), F32)
    zp_sc = jnp.zeros((bp, SC_WIDTH - 1, D_SC), F32)
    states = [[] for _ in range(6)]
    for l in range(depth):
        mix_w = (g_mix[l][None, :], w_in[l].astype(BF16), _block_diag(w_pool[l]).astype(BF16), pool_scale[l][None, :],
                 w_dw[l], b_dw[l][None, :], ln_g[l][None, :], ln_b[l][None, :], w_sc[l], w_out[l].astype(BF16))
        wq_hi, wq_lo = _split_bf16(w_q[l].T)
        k_hi, k_lo = _split_bf16(sub_keys[l])
        peer_w = (g_ffn[l][None, :], wq_hi, wq_lo, k_hi, k_lo, _pack_table(expert_u[l]), _pack_table(expert_v[l]))

        xp, a, b, c = _mixer(xp, zp_pool, zp_dw, zp_sc, 0, *mix_w, L=tiles_p["mixer"])
        xp = _peer(xp.reshape(bp * sp, D_MODEL), *peer_w, tiles_p).reshape(bp, sp, D_MODEL)
        states[0].append(a); states[1].append(b); states[2].append(c)

        xs, a, b, c = _mixer(xs, state_pool[l], state_dwconv[l], state_shortconv[l], PAST_LEN, *mix_w, L=tiles_s["mixer"])
        xs = _peer(xs.reshape(bs * ss, D_MODEL), *peer_w, tiles_s).reshape(bs, ss, D_MODEL)
        states[3].append(a); states[4].append(b); states[5].append(c)

    y_prompt = _final_norm(xp.reshape(bp * sp, D_MODEL), g_final[None, :], tiles_p["norm"]).reshape(bp, sp, D_MODEL)
    y_sample = _final_norm(xs.reshape(bs * ss, D_MODEL), g_final[None, :], tiles_s["norm"]).reshape(bs, ss, D_MODEL)
    return (y_prompt, y_sample) + tuple(jnp.stack(s) for s in states)
```
